```python
import jax, jax.numpy as jnp
from jax import lax
import numpy as np

D_MODEL = 1024
BATCH = 8
SEQ = 8192
DEPTH = 4
DEC_BATCH = 32
DEC_SEQ = 32
PAST_LEN = 2048

CHUNK = 64
QBLK = 128
H_A = 4
HD_A = 64
W_A = H_A * HD_A
H_I = 4
D_I = 32
TOPK_MAX = 256
H_B = 4
DK_B = 48
DV_B = 96
W_B = H_B * DV_B
GATE_RANK = 16
GLA_TAU = 16.0
H_C = 4
HD_C = 96
W_C = H_C * HD_C
CONV_W = 4
D_MIX = W_A + W_B + W_C
N_EXPERTS = 16
N_GROUPS = 4
EXPERTS_PER_GROUP = N_EXPERTS // N_GROUPS
TOP_K_EXPERTS = 2
D_EXPERT = 256
ALPHA = float((2 * DEPTH) ** 0.25)
BETA = float((8 * DEPTH) ** -0.25)
LN_EPS = 1e-5
_IN_SIZES = (W_A, W_A, W_A, H_I * D_I, D_I, H_I, H_B * DK_B, H_B * DK_B, W_B, W_B, GATE_RANK, W_C, W_C, W_C)
P_IN = sum(_IN_SIZES)
_IN_SPLITS = tuple(int(s) for s in np.cumsum(_IN_SIZES)[:-1])
F32 = jnp.float32

kernel_name = 'hybrid_dsa_gla_mlstm_moe_stream_step'


def layer_norm(x, g, b):
    xf = x.astype(F32)
    mu = jnp.mean(xf, axis=-1, keepdims=True)
    var = jnp.mean(jnp.square(xf - mu), axis=-1, keepdims=True)
    return ((xf - mu) * lax.rsqrt(var + LN_EPS) * g.astype(F32) + b.astype(F32)).astype(x.dtype)


def head_rms_norm(h, g):
    hf = h.astype(F32)
    hf = hf * lax.rsqrt(jnp.mean(hf * hf, axis=-1, keepdims=True) + LN_EPS)
    return hf.reshape(*h.shape[:-2], -1) * g.astype(F32)


def causal_dwconv(xc, w, b):
    out = lax.conv_general_dilated(xc, w.astype(xc.dtype)[:, None, :], window_strides=(1,), padding='VALID',
                                   dimension_numbers=('NWC', 'WIO', 'NWC'), feature_group_count=xc.shape[-1])
    return out + b.astype(out.dtype)


def _to_chunks(a, nc):
    return a.reshape(a.shape[0], nc, CHUNK, *a.shape[2:]).swapaxes(0, 1)


def _from_chunks(a):
    return a.swapaxes(0, 1).reshape(a.shape[1], a.shape[0] * a.shape[2], *a.shape[3:])


def _dsa_attend(q, qi, wi, qpos, k, v, ki, topk):
    S = k.shape[1]
    qchunk = qpos // CHUNK
    admissible = (jnp.arange(S)[None, :] // CHUNK) <= qchunk[:, None]
    rel = jax.nn.relu(jnp.einsum('bthd,bsd->bths', qi.astype(F32), ki.astype(F32)) * D_I ** -0.5)
    iscore = jnp.einsum('bth,bths->bts', wi.astype(F32) * H_I ** -0.5, rel)
    iscore = jnp.where(admissible[None], iscore, -jnp.inf)
    _, sel = lax.top_k(iscore, topk)
    valid = (sel // CHUNK) <= qchunk[None, :, None]
    gather = jax.vmap(lambda a, i: a[i])
    kg = gather(k, sel).astype(F32)
    vg = gather(v, sel).astype(F32)
    logits = jnp.einsum('bthd,btjhd->bthj', q.astype(F32), kg) * HD_A ** -0.5
    logits = jnp.where(valid[:, :, None, :], logits, -jnp.inf)
    p = jax.nn.softmax(logits, axis=-1)
    return jnp.einsum('bthj,btjhd->bthd', p, vg)


def dsa_prompt(q, qi, wi, k, v, ki):
    B, S = q.shape[:2]
    topk = min(TOPK_MAX, S // 4)
    def blk(i):
        sl = lambda a: lax.dynamic_slice_in_dim(a, i * QBLK, QBLK, axis=1)
        return _dsa_attend(sl(q), sl(qi), sl(wi), i * QBLK + jnp.arange(QBLK), k, v, ki, topk)
    out = lax.map(blk, jnp.arange(S // QBLK))
    return _from_chunks_q(out)


def _from_chunks_q(a):
    return a.swapaxes(0, 1).reshape(a.shape[1], a.shape[0] * a.shape[2], *a.shape[3:])


def dsa_sample(q, qi, wi, k, v, ki, ck, cv, cki):
    P, T = ck.shape[1], q.shape[1]
    kk = jnp.concatenate([ck.astype(F32), k.astype(F32)], axis=1)
    vv = jnp.concatenate([cv.astype(F32), v.astype(F32)], axis=1)
    kki = jnp.concatenate([cki.astype(F32), ki.astype(F32)], axis=1)
    topk = min(TOPK_MAX, (P + T) // 4)
    return _dsa_attend(q, qi, wi, P + jnp.arange(T), kk, vv, kki, topk)


def gla_chunk(q, k, v, lg, S0):
    L = q.shape[1]
    S0 = S0.astype(F32)
    b = jnp.cumsum(lg, axis=1)
    causal = jnp.tril(jnp.ones((L, L), bool))
    dec = jnp.exp(jnp.where(causal[None, :, :, None, None], b[:, :, None] - b[:, None, :], -jnp.inf))
    att = jnp.einsum('bthc,bshc,btshc->bhts', q, k, dec)
    o = jnp.einsum('bhts,bshv->bthv', att, v) + jnp.einsum('bthc,bhcv->bthv', q * jnp.exp(b), S0)
    bL = b[:, -1]
    S1 = jnp.exp(bL)[..., None] * S0 + jnp.einsum('bshc,bshv->bhcv', k * jnp.exp(bL[:, None] - b), v)
    return o, S1


def gla_prompt(q, k, v, lg):
    B, S = q.shape[:2]
    nc = S // CHUNK
    def step(St, inp):
        o, S1 = gla_chunk(*inp, St)
        return S1, o
    S_fin, o = lax.scan(step, jnp.zeros((B, H_B, DK_B, DV_B), F32),
                        (_to_chunks(q, nc), _to_chunks(k, nc), _to_chunks(v, nc), _to_chunks(lg, nc)))
    return _from_chunks(o), S_fin


def mlstm_chunk(q, k, v, ig, lf, C0, n0, m0):
    L = q.shape[1]
    C0, n0, m0 = C0.astype(F32), n0.astype(F32), m0.astype(F32)
    Fc = jnp.cumsum(lf, axis=1)
    a = Fc + m0[:, None]
    causal = jnp.tril(jnp.ones((L, L), bool))
    Dm = jnp.where(causal[None, :, :, None], Fc[:, :, None] - Fc[:, None, :] + ig[:, None, :], -jnp.inf)
    m = jnp.maximum(a, jnp.max(Dm, axis=2))
    w_in = jnp.exp(Dm - m[:, :, None])
    w_st = jnp.exp(a - m)
    qk = jnp.einsum('bthd,bshd->btsh', q, k) * w_in
    num = jnp.einsum('btsh,bshe->bthe', qk, v) + w_st[..., None] * jnp.einsum('bthd,bhde->bthe', q, C0)
    den = jnp.sum(qk, axis=2) + w_st * jnp.einsum('bthd,bhd->bth', q, n0)
    h = num / jnp.maximum(jnp.abs(den), jnp.exp(-m))[..., None]
    mL = m[:, -1]
    wL = jnp.exp(Dm[:, -1] - mL[:, None])
    gL = jnp.exp(a[:, -1] - mL)
    C1 = gL[..., None, None] * C0 + jnp.einsum('bsh,bshd,bshe->bhde', wL, k, v)
    n1 = gL[..., None] * n0 + jnp.einsum('bsh,bshd->bhd', wL, k)
    return h, C1, n1, mL


def mlstm_prompt(q, k, v, ig, lf):
    B, S = q.shape[:2]
    nc = S // CHUNK
    init = (jnp.zeros((B, H_C, HD_C, HD_C), F32), jnp.zeros((B, H_C, HD_C), F32), jnp.zeros((B, H_C), F32))
    def step(carry, inp):
        h, C1, n1, m1 = mlstm_chunk(*inp, *carry)
        return (C1, n1, m1), h
    (C, n, m), h = lax.scan(step, init, (_to_chunks(q, nc), _to_chunks(k, nc), _to_chunks(v, nc),
                                          _to_chunks(ig, nc), _to_chunks(lf, nc)))
    return _from_chunks(h), C, n, m


def token_mixer(x, w_in, w_gla_a2, b_gla_a, g_gla, conv_w, conv_b, w_mq, w_mk, w_if, b_if, g_mlstm, w_out, past):
    B, T, _ = x.shape
    (qa, ka, va, qi, ki, wi, qb, kb, vb, rb, glr, cc, vc, oc) = jnp.split(x @ w_in, _IN_SPLITS, axis=-1)
    heads = lambda a, h: a.reshape(B, T, h, -1)
    qa, ka, va, qi = heads(qa, H_A), heads(ka, H_A), heads(va, H_A), heads(qi, H_I)
    lg = (jax.nn.log_sigmoid((glr @ w_gla_a2 + b_gla_a).astype(F32)) / GLA_TAU).reshape(B, T, H_B, DK_B)
    qb = heads(qb, H_B).astype(F32) * DK_B ** -0.5
    kb = heads(kb, H_B).astype(F32)
    vb = heads(vb, H_B).astype(F32)
    if past is None:
        buf = jnp.zeros((B, CONV_W - 1, W_C), cc.dtype)
    else:
        ck, cv, cki, s_gla, c_m, n_m, m_m, buf = past
    xc = jnp.concatenate([buf.astype(cc.dtype), cc], axis=1)
    conv_new = xc[:, -(CONV_W - 1):]
    u = jax.nn.silu(causal_dwconv(xc, conv_w, conv_b).astype(F32)).reshape(B, T, H_C, HD_C)
    qc = jnp.einsum('bthd,hde->bthe', u, w_mq.astype(F32))
    kc = jnp.einsum('bthd,hde->bthe', u, w_mk.astype(F32)) * HD_C ** -0.5
    vch = heads(vc, H_C).astype(F32)
    gates = jnp.concatenate([qc.reshape(B, T, W_C), kc.reshape(B, T, W_C), vch.reshape(B, T, W_C)], axis=-1) \
        @ w_if.astype(F32) + b_if.astype(F32)
    ig = gates[..., :H_C]
    lf = jax.nn.log_sigmoid(gates[..., H_C:])
    if past is None:
        ya = dsa_prompt(qa, qi, wi, ka, va, ki)
        ob, s_new = gla_prompt(qb, kb, vb, lg)
        hc, c_new, n_new, m_new = mlstm_prompt(qc, kc, vch, ig, lf)
    else:
        ya = dsa_sample(qa, qi, wi, ka, va, ki, ck, cv, cki)
        ob, s_new = gla_chunk(qb, kb, vb, lg, s_gla)
        hc, c_new, n_new, m_new = mlstm_chunk(qc, kc, vch, ig, lf, c_m, n_m, m_m)
    yb = head_rms_norm(ob, g_gla) * jax.nn.silu(rb.astype(F32))
    yc = head_rms_norm(jax.nn.sigmoid(oc.astype(F32)).reshape(B, T, H_C, HD_C) * hc, g_mlstm)
    y = jnp.concatenate([ya.reshape(B, T, W_A), yb, yc], axis=-1) @ w_out.astype(F32)
    return y.astype(x.dtype), (ka, va, ki, s_new, c_new, n_new, m_new, conv_new)


def moe(x, w_router, b_router, w_gate, w_up, w_down):
    B, T, D = x.shape
    xf = x.reshape(B * T, D)
    s = jax.nn.sigmoid((xf @ w_router).astype(F32))
    sel = s + b_router.astype(F32)
    gscore = jnp.sum(lax.top_k(sel.reshape(-1, N_GROUPS, EXPERTS_PER_GROUP), TOP_K_EXPERTS)[0], axis=-1)
    gbest = jnp.argmax(gscore, axis=-1)
    in_group = (jnp.arange(N_EXPERTS) // EXPERTS_PER_GROUP)[None, :] == gbest[:, None]
    _, eidx = lax.top_k(jnp.where(in_group, sel, -jnp.inf), TOP_K_EXPERTS)
    wsel = jnp.take_along_axis(s, eidx, axis=-1)
    wsel = wsel / jnp.sum(wsel, axis=-1, keepdims=True)
    gate = jnp.einsum('nk,nke->ne', wsel, jax.nn.one_hot(eidx, N_EXPERTS, dtype=F32))
    out = jnp.zeros((B * T, D), F32)
    for e in range(N_EXPERTS):
        h = jax.nn.silu(xf @ w_gate[e]) * (xf @ w_up[e])
        out = out + gate[:, e:e + 1] * (h @ w_down[e]).astype(F32)
    return out.reshape(B, T, D).astype(x.dtype)


def trunk_layer(x, lw, fw, past):
    y, st = token_mixer(x, *lw, past)
    ln1_g, ln1_b, w_router, b_router, wg, wu, wd, ln2_g, ln2_b = fw
    x = layer_norm(ALPHA * x + y, ln1_g, ln1_b)
    x = layer_norm(ALPHA * x + moe(x, w_router, b_router, wg, wu, wd), ln2_g, ln2_b)
    return x, st


def setup_inputs(seed: int = 0) -> dict:
    key = jax.random.key(seed)
    ks = iter(jax.random.split(key, 40))
    nrm = lambda shape, scale: jax.random.normal(next(ks), shape, F32) * scale
    b_if = jnp.concatenate([nrm((DEPTH, H_C), 0.1),
                            jnp.linspace(3.0, 6.0, H_C, dtype=F32)[None] + nrm((DEPTH, H_C), 0.1)], axis=-1)
    return {
        'x_prompt': nrm((BATCH, SEQ, D_MODEL), 1.0),
        'x_sample': nrm((DEC_BATCH, DEC_SEQ, D_MODEL), 1.0),
        'cache_k': nrm((DEPTH, DEC_BATCH, PAST_LEN, H_A, HD_A), 1.0),
        'cache_v': nrm((DEPTH, DEC_BATCH, PAST_LEN, H_A, HD_A), 1.0),
        'cache_kidx': nrm((DEPTH, DEC_BATCH, PAST_LEN, D_I), 1.0),
        'state_gla': nrm((DEPTH, DEC_BATCH, H_B, DK_B, DV_B), 0.5),
        'state_mlstm_C': nrm((DEPTH, DEC_BATCH, H_C, HD_C, HD_C), 0.5),
        'state_mlstm_n': nrm((DEPTH, DEC_BATCH, H_C, HD_C), 0.5),
        'state_mlstm_m': nrm((DEPTH, DEC_BATCH, H_C), 0.5),
        'state_conv': nrm((DEPTH, DEC_BATCH, CONV_W - 1, W_C), 1.0),
        'ln0_g': 1.0 + nrm((D_MODEL,), 0.02),
        'ln0_b': nrm((D_MODEL,), 0.02),
        'w_in': nrm((DEPTH, D_MODEL, P_IN), D_MODEL ** -0.5),
        'w_gla_a2': nrm((DEPTH, GATE_RANK, H_B * DK_B), GATE_RANK ** -0.5),
        'b_gla_a': nrm((DEPTH, H_B * DK_B), 0.1),
        'g_gla': 1.0 + nrm((DEPTH, W_B), 0.02),
        'conv_w': nrm((DEPTH, CONV_W, W_C), CONV_W ** -0.5),
        'conv_b': nrm((DEPTH, W_C), 0.02),
        'w_mq': nrm((DEPTH, H_C, HD_C, HD_C), HD_C ** -0.5),
        'w_mk': nrm((DEPTH, H_C, HD_C, HD_C), HD_C ** -0.5),
        'w_if': nrm((DEPTH, 3 * W_C, 2 * H_C), 0.1 * (3 * W_C) ** -0.5),
        'b_if': b_if,
        'g_mlstm': 1.0 + nrm((DEPTH, W_C), 0.02),
        'w_out': nrm((DEPTH, D_MIX, D_MODEL), BETA * D_MIX ** -0.5),
        'ln1_g': 1.0 + nrm((DEPTH, D_MODEL), 0.02),
        'ln1_b': nrm((DEPTH, D_MODEL), 0.02),
        'w_router': nrm((D_MODEL, N_EXPERTS), D_MODEL ** -0.5),
        'b_router': nrm((N_EXPERTS,), 0.01),
        'w_exp_gate': nrm((DEPTH, N_EXPERTS, D_MODEL, D_EXPERT), D_MODEL ** -0.5),
        'w_exp_up': nrm((DEPTH, N_EXPERTS, D_MODEL, D_EXPERT), D_MODEL ** -0.5),
        'w_exp_down': nrm((DEPTH, N_EXPERTS, D_EXPERT, D_MODEL), BETA * D_EXPERT ** -0.5),
        'ln2_g': 1.0 + nrm((DEPTH, D_MODEL), 0.02),
        'ln2_b': nrm((DEPTH, D_MODEL), 0.02),
    }


def reference(x_prompt, x_sample, cache_k, cache_v, cache_kidx, state_gla, state_mlstm_C, state_mlstm_n,
              state_mlstm_m, state_conv, ln0_g, ln0_b, w_in, w_gla_a2, b_gla_a, g_gla, conv_w, conv_b, w_mq, w_mk,
              w_if, b_if, g_mlstm, w_out, ln1_g, ln1_b, w_router, b_router, w_exp_gate, w_exp_up, w_exp_down,
              ln2_g, ln2_b):
    xp = layer_norm(x_prompt, ln0_g, ln0_b)
    xs = layer_norm(x_sample, ln0_g, ln0_b)
    sp, ss = [], []
    for l in range(DEPTH):
        lw = (w_in[l], w_gla_a2[l], b_gla_a[l], g_gla[l], conv_w[l], conv_b[l], w_mq[l], w_mk[l],
              w_if[l], b_if[l], g_mlstm[l], w_out[l])
        fw = (ln1_g[l], ln1_b[l], w_router, b_router, w_exp_gate[l], w_exp_up[l], w_exp_down[l], ln2_g[l], ln2_b[l])
        xp, st_p = trunk_layer(xp, lw, fw, None)
        past = (cache_k[l], cache_v[l], cache_kidx[l], state_gla[l], state_mlstm_C[l], state_mlstm_n[l],
                state_mlstm_m[l], state_conv[l])
        xs, st_s = trunk_layer(xs, lw, fw, past)
        sp.append(st_p)
        ss.append(st_s)
    stk = lambda sts, i: jnp.stack([s[i] for s in sts])
    return (xp, xs,
            stk(sp, 0), stk(sp, 1), stk(sp, 2), stk(sp, 3), stk(sp, 4), stk(sp, 5), stk(sp, 6), stk(sp, 7),
            stk(ss, 0), stk(ss, 1), stk(ss, 2), stk(ss, 3), stk(ss, 4), stk(ss, 5), stk(ss, 6), stk(ss, 7))
```

```python
import functools

import numpy as np
import jax
import jax.numpy as jnp
from jax import lax
from jax.experimental import pallas as pl
from jax.experimental.pallas import tpu as pltpu

F32 = jnp.float32
BF16 = jnp.bfloat16
I32 = jnp.int32

D_MODEL = 1024
CHUNK = 64
H_A, HD_A = 4, 64
W_A = H_A * HD_A
H_I, D_I = 4, 32
TOPK_MAX = 256
H_B, DK_B, DV_B = 4, 48, 96
W_B = H_B * DV_B
GATE_RANK = 16
GLA_TAU = 16.0
H_C, HD_C = 4, 96
W_C = H_C * HD_C
CONV_W = 4
N_EXPERTS = 16
N_GROUPS = 4
EXPERTS_PER_GROUP = N_EXPERTS // N_GROUPS
D_EXPERT = 256
DEPTH_SCALE_LAYERS = 4
ALPHA = float((2 * DEPTH_SCALE_LAYERS) ** 0.25)
LN_EPS = 1e-5
_IN_SIZES = (W_A, W_A, W_A, H_I * D_I, D_I, H_I, H_B * DK_B, H_B * DK_B, W_B, W_B, GATE_RANK, W_C, W_C, W_C)
_IN_OFFS = tuple(int(v) for v in np.cumsum((0,) + _IN_SIZES))

LANE = 128
HP = 4 * LANE
VMEM_LIMIT = 56 * 1024 * 1024
NEG = -1e30
INT_MIN = -2 ** 31
HI = lax.Precision.HIGHEST


def _dot(a, b, precision=None):
    return jnp.dot(a, b, preferred_element_type=F32, precision=precision)


def _dot_nt(a, b, precision=None):
    return lax.dot_general(a, b, (((1,), (1,)), ((), ())), preferred_element_type=F32, precision=precision)


def _dot_tn(a, b, precision=None):
    return lax.dot_general(a, b, (((0,), (0,)), ((), ())), preferred_element_type=F32, precision=precision)


def _log_sigmoid(x):
    return jnp.minimum(x, 0.0) - jnp.log1p(jnp.exp(-jnp.abs(x)))


def _sigmoid(x):
    return 1.0 / (1.0 + jnp.exp(-x))


def _layer_norm(x, g, b):
    mu = jnp.mean(x, axis=-1, keepdims=True)
    xc = x - mu
    var = jnp.mean(xc * xc, axis=-1, keepdims=True)
    return xc * lax.rsqrt(var + LN_EPS) * g + b


def _params(sem):
    return pltpu.CompilerParams(dimension_semantics=sem, vmem_limit_bytes=VMEM_LIMIT)


def _full(shape):
    n = len(shape)
    return pl.BlockSpec(shape, lambda *_: (0,) * n)


def _ln_kernel(x_ref, g_ref, b_ref, o_ref):
    o_ref[...] = _layer_norm(x_ref[...], g_ref[...], b_ref[...])


def _ln0(x, g, b, tm):
    n = x.shape[0]
    return pl.pallas_call(
        _ln_kernel,
        grid=(n // tm,),
        in_specs=[pl.BlockSpec((tm, D_MODEL), lambda i: (i, 0)), _full((1, D_MODEL)), _full((1, D_MODEL))],
        out_specs=pl.BlockSpec((tm, D_MODEL), lambda i: (i, 0)),
        out_shape=jax.ShapeDtypeStruct((n, D_MODEL), F32),
        compiler_params=_params(("parallel",)),
        name="ln0",
    )(x, g.reshape(1, -1), b.reshape(1, -1))


_PROJ_OUT = (
    ("ka", W_A, F32), ("va", W_A, F32), ("ki", D_I, F32),
    ("qab", W_A, BF16), ("kab", W_A, BF16), ("vab", W_A, BF16),
    ("qib", H_I * D_I, BF16), ("kib", D_I, BF16), ("wi", LANE, F32),
    ("qb", HP, F32), ("kb", HP, F32), ("vb", HP, F32), ("rb", HP, F32), ("lg", HP, F32),
    ("cc", W_C, F32), ("vc", HP, F32), ("oc", HP, F32),
)


def _proj_kernel(x_ref, wa_ref, wi_ref, wb_ref, wa2_ref, ba_ref, wc_ref, *outs):
    o = dict(zip([n for n, _, _ in _PROJ_OUT], outs))
    xb = x_ref[...].astype(BF16)
    za = _dot(xb, wa_ref[...])
    o["qab"][...] = (za[:, :W_A] * (HD_A ** -0.5)).astype(BF16)
    ka, va = za[:, W_A:2 * W_A], za[:, 2 * W_A:]
    o["ka"][...] = ka
    o["va"][...] = va
    o["kab"][...] = ka.astype(BF16)
    o["vab"][...] = va.astype(BF16)
    zi = _dot(xb, wi_ref[...])
    o["qib"][...] = zi[:, :LANE].astype(BF16)
    ki = zi[:, LANE:LANE + D_I]
    o["ki"][...] = ki
    o["kib"][...] = ki.astype(BF16)
    o["wi"][...] = (zi[:, 2 * LANE:] * (H_I ** -0.5)).astype(BF16).astype(F32)
    zb = _dot(xb, wb_ref[...])
    o["qb"][...] = zb[:, :HP] * (DK_B ** -0.5)
    o["kb"][...] = zb[:, HP:2 * HP]
    o["vb"][...] = zb[:, 2 * HP:3 * HP]
    o["rb"][...] = zb[:, 3 * HP:4 * HP]
    glr = zb[:, 4 * HP:].astype(BF16)
    o["lg"][...] = _log_sigmoid(_dot(glr, wa2_ref[...]) + ba_ref[...]) * (1.0 / GLA_TAU)
    zc = _dot(xb, wc_ref[...])
    o["cc"][...] = zc[:, :W_C]
    o["vc"][...] = zc[:, W_C:W_C + HP]
    o["oc"][...] = zc[:, W_C + HP:]


def _proj(x, pw, tm):
    n = x.shape[0]
    row = lambda w: pl.BlockSpec((tm, w), lambda i: (i, 0))
    ws = (pw["wa"], pw["wi"], pw["wb"], pw["wa2"], pw["ba"], pw["wc"])
    outs = pl.pallas_call(
        _proj_kernel,
        grid=(n // tm,),
        in_specs=[row(D_MODEL)] + [_full(w.shape) for w in ws],
        out_specs=[row(w) for _, w, _ in _PROJ_OUT],
        out_shape=[jax.ShapeDtypeStruct((n, w), d) for _, w, d in _PROJ_OUT],
        compiler_params=_params(("parallel",)),
        name="proj",
    )(x, *ws)
    return dict(zip([nm for nm, _, _ in _PROJ_OUT], outs))


def _pad_heads(w, heads, d):
    lead = w.shape[:-1]
    w = w.reshape(*lead, heads, d)
    w = jnp.pad(w, [(0, 0)] * len(lead) + [(0, 0), (0, LANE - d)])
    return w.reshape(*lead, heads * LANE)


def _pad_last(w, to):
    return jnp.pad(w, [(0, 0)] * (w.ndim - 1) + [(0, to - w.shape[-1])])


def _proj_weights(w_in, w_gla_a2, b_gla_a):
    c = lambda i: w_in[:, _IN_OFFS[i]:_IN_OFFS[i + 1]]
    wa = jnp.concatenate([c(0), c(1), c(2)], axis=1)
    wi = jnp.concatenate([c(3), _pad_last(c(4), LANE), _pad_last(c(5), LANE)], axis=1)
    wb = jnp.concatenate([_pad_heads(c(6), H_B, DK_B), _pad_heads(c(7), H_B, DK_B), _pad_heads(c(8), H_B, DV_B),
                          _pad_heads(c(9), H_B, DV_B), _pad_last(c(10), LANE)], axis=1)
    wc = jnp.concatenate([c(11), _pad_heads(c(12), H_C, HD_C), _pad_heads(c(13), H_C, HD_C)], axis=1)
    wa2 = jnp.pad(_pad_heads(w_gla_a2, H_B, DK_B), ((0, LANE - GATE_RANK), (0, 0)))
    ba = _pad_heads(b_gla_a.reshape(1, -1), H_B, DK_B)
    return dict(wa=wa.astype(BF16), wi=wi.astype(BF16), wb=wb.astype(BF16), wc=wc.astype(BF16),
                wa2=wa2.astype(BF16), ba=ba.astype(F32))


def _fold8(x):
    r = x.shape[0] // 8
    parts = [x[8 * j:8 * j + 8] for j in range(r)]
    acc = parts[:4]
    for j in range(4, r):
        acc[j % 4] = acc[j % 4] + parts[j]
    while len(acc) > 1:
        acc = [acc[a] + acc[a + 1] for a in range(0, len(acc) - 1, 2)] + ([acc[-1]] if len(acc) % 2 else [])
    return acc[0]


def _dsa_kernel(qt_ref, qit_ref, wit_ref, k_ref, vt_ref, ki_ref, ltri_ref, o_ref,
                keys_ref, m_ref, l_ref, acc_ref, *, q_offset, kv_len, tq, tk, topk):
    i = pl.program_id(1)
    t0 = q_offset + i * tq
    qpos = t0 + lax.broadcasted_iota(I32, (1, tq), 1)
    lim = jnp.minimum(((qpos >> 6) + 1) * CHUNK, kv_len)
    last_lim = jnp.minimum((((t0 + tq - 1) >> 6) + 1) * CHUNK, kv_len)
    n_tiles = (last_lim + tk - 1) // tk

    qit = qit_ref[0, 0]
    wit = wit_ref[0, 0]
    row_in_tile = lax.broadcasted_iota(I32, (tk, tq), 0)

    def score_tile(kt, carry):
        kit = ki_ref[0, pl.ds(pl.multiple_of(kt * tk, tk), tk), :]
        acc = jnp.zeros((tk, tq), F32)
        for h in range(H_I):
            d = _dot(kit, qit[D_I * h:D_I * (h + 1), :])
            rel = (jnp.maximum(d, 0.0) * (D_I ** -0.5)).astype(BF16).astype(F32)
            acc = acc + wit[h:h + 1, :] * rel
        bits = pltpu.bitcast(acc, I32)
        key = bits ^ ((bits >> 31) & 0x7FFFFFFF)
        key = jnp.where(row_in_tile + kt * tk < lim, key, INT_MIN)
        keys_ref[kt] = key
        return carry

    lax.fori_loop(0, n_tiles, score_tile, 0)

    def count_ge(cand):
        def body(kt, c):
            return c + _fold8(jnp.where(keys_ref[kt] >= cand, 1, 0))
        c = lax.fori_loop(0, n_tiles, body, jnp.zeros((8, tq), I32))
        return jnp.sum(c, axis=0, keepdims=True)

    tau0 = jnp.full((1, tq), INT_MIN, I32)
    cand0 = jnp.zeros((1, tq), I32)
    tau = jnp.where(count_ge(cand0) >= topk, cand0, tau0)

    def bisect(bi, tau):
        cand = tau + lax.shift_left(jnp.int32(1), 30 - bi)
        return jnp.where(count_ge(cand) >= topk, cand, tau)

    tau = lax.fori_loop(0, 31, bisect, tau)
    n_gt = count_ge(tau + 1)
    need = jnp.where(tau == INT_MIN, 0, topk - n_gt).astype(F32)

    qt = qt_ref[0, 0]
    head_of_row = lax.broadcasted_iota(I32, (W_A, tq), 0) // HD_A
    qf = qt.astype(F32)
    qmask = [jnp.where(head_of_row == h, qf, 0.0).astype(BF16) for h in range(H_A)]
    m_ref[...] = jnp.full(m_ref.shape, NEG, F32)
    l_ref[...] = jnp.zeros(l_ref.shape, F32)
    acc_ref[...] = jnp.zeros(acc_ref.shape, F32)
    ltri = ltri_ref[...]

    def attend_tile(kt, tie_carry):
        key = keys_ref[kt]
        tie = key == tau
        prefix = _dot(ltri, jnp.where(tie, 1.0, 0.0).astype(BF16)) + tie_carry
        sel = (key > tau) | (tie & (prefix <= need))
        ktile = k_ref[0, pl.ds(pl.multiple_of(kt * tk, tk), tk), :]
        vt = vt_ref[0, kt]
        for h in range(H_A):
            s = jnp.where(sel, _dot(ktile, qmask[h]), NEG)
            m_old = m_ref[h:h + 1, :]
            m_new = jnp.maximum(m_old, jnp.max(s, axis=0, keepdims=True))
            alpha = jnp.exp(m_old - m_new)
            p = jnp.exp(s - m_new)
            l_ref[h:h + 1, :] = alpha * l_ref[h:h + 1, :] + jnp.sum(p, axis=0, keepdims=True)
            pv = _dot(vt[HD_A * h:HD_A * (h + 1), :], p.astype(BF16))
            acc_ref[HD_A * h:HD_A * (h + 1), :] = alpha * acc_ref[HD_A * h:HD_A * (h + 1), :] + pv
            m_ref[h:h + 1, :] = m_new
        return prefix[tk - 1:tk, :]

    lax.fori_loop(0, n_tiles, attend_tile, jnp.zeros((1, tq), F32))
    for h in range(H_A):
        o_ref[0, 0, HD_A * h:HD_A * (h + 1), :] = acc_ref[HD_A * h:HD_A * (h + 1), :] / l_ref[h:h + 1, :]


def _dsa(qab, qib, wi, kab, vab, kib, *, q_offset, kv_len, tq, tk):
    b, t, _ = qab.shape
    s = kab.shape[1]
    nq, nt = t // tq, s // tk
    topk = min(TOPK_MAX, kv_len // 4)
    tr = lambda a: a.reshape(b, nq, tq, a.shape[-1]).transpose(0, 1, 3, 2)
    qt, qit, wit = tr(qab), tr(qib), tr(wi[..., :8])
    vt = vab.reshape(b, nt, tk, W_A).transpose(0, 1, 3, 2)
    ltri = jnp.asarray(np.tril(np.ones((tk, tk), np.float32)), BF16)
    qspec = lambda r: pl.BlockSpec((1, 1, r, tq), lambda bi, i: (bi, i, 0, 0))
    out = pl.pallas_call(
        functools.partial(_dsa_kernel, q_offset=q_offset, kv_len=kv_len, tq=tq, tk=tk, topk=topk),
        grid=(b, nq),
        in_specs=[qspec(W_A), qspec(H_I * D_I), qspec(8),
                  pl.BlockSpec((1, s, W_A), lambda bi, i: (bi, 0, 0)),
                  pl.BlockSpec((1, nt, W_A, tk), lambda bi, i: (bi, 0, 0, 0)),
                  pl.BlockSpec((1, s, D_I), lambda bi, i: (bi, 0, 0)),
                  _full((tk, tk))],
        out_specs=qspec(W_A),
        out_shape=jax.ShapeDtypeStruct((b, nq, W_A, tq), F32),
        scratch_shapes=[pltpu.VMEM((nt, tk, tq), I32), pltpu.VMEM((8, tq), F32), pltpu.VMEM((8, tq), F32),
                        pltpu.VMEM((W_A, tq), F32)],
        compiler_params=_params(("parallel", "arbitrary")),
        name="dsa",
    )(qt, qit, wit, kab, vt, kib, ltri)
    return out.transpose(0, 1, 3, 2).reshape(b, t, W_A)


GLA_SUB = 16


def _gla_kernel(q_ref, k_ref, v_ref, lg_ref, s0_ref, tril_ref, msub_ref, o_ref, s_out_ref, st_ref, *, L):
    j = pl.program_id(1)

    @pl.when(j == 0)
    def _():
        st_ref[...] = s0_ref[0]

    q, k, v, lg = q_ref[0], k_ref[0], v_ref[0], lg_ref[0]
    b = _dot(tril_ref[...], lg, HI)
    bq = _dot(msub_ref[...], lg, HI)
    qsub = q * jnp.exp(bq)
    qin = (q * jnp.exp(b)).astype(BF16)
    b_last = b[L - 1:L, :]
    kst = (k * jnp.exp(b_last - b)).astype(BF16)
    vb = v.astype(BF16)
    rows = lax.broadcasted_iota(I32, (L, 1), 0)
    nsub = L // GLA_SUB
    ke = []
    for i in range(nsub):
        ref_i = b[GLA_SUB * i - 1:GLA_SUB * i, :] if i > 0 else jnp.zeros((1, HP), F32)
        e = jnp.where(rows < GLA_SUB * (i + 1), ref_i - b, NEG)
        ke.append(k * jnp.exp(e))
    causal = lax.broadcasted_iota(I32, (L, L), 0) >= lax.broadcasted_iota(I32, (L, L), 1)
    for h in range(H_B):
        sl = slice(LANE * h, LANE * (h + 1))
        att = jnp.concatenate(
            [_dot_nt(qsub[GLA_SUB * i:GLA_SUB * (i + 1), sl], ke[i][:, sl], HI) for i in range(nsub)], axis=0)
        att = jnp.where(causal, att, 0.0).astype(BF16)
        st = st_ref[h]
        o_ref[0, :, sl] = _dot(att, vb[:, sl]) + _dot_nt(qin[:, sl], st.astype(BF16))
        st_ref[h] = st * jnp.exp(b_last[:, sl]) + _dot_tn(vb[:, sl], kst[:, sl])

    @pl.when(j == pl.num_programs(1) - 1)
    def _():
        s_out_ref[0] = st_ref[...]


def _gla(qb, kb, vb, lg, s0t, L):
    b, t, _ = qb.shape
    nc = t // L
    tril = np.tril(np.ones((L, L), np.float32))
    idx = np.arange(L)
    msub = tril * ((idx[:, None] // GLA_SUB) == (idx[None, :] // GLA_SUB))
    tok = pl.BlockSpec((1, L, HP), lambda bi, j: (bi, j, 0))
    st = pl.BlockSpec((1, H_B, LANE, LANE), lambda bi, j: (bi, 0, 0, 0))
    return pl.pallas_call(
        functools.partial(_gla_kernel, L=L),
        grid=(b, nc),
        in_specs=[tok, tok, tok, tok, st, _full((L, L)), _full((L, L))],
        out_specs=[tok, st],
        out_shape=[jax.ShapeDtypeStruct((b, t, HP), F32), jax.ShapeDtypeStruct((b, H_B, LANE, LANE), F32)],
        scratch_shapes=[pltpu.VMEM((H_B, LANE, LANE), F32)],
        compiler_params=_params(("parallel", "arbitrary")),
        name="gla",
    )(qb, kb, vb, lg, s0t, jnp.asarray(tril), jnp.asarray(msub.astype(np.float32)))


CONV_PAD = 8


def _mlstm_kernel(cc_ref, vc_ref, conv0_ref, c0_ref, m0_ref, cw_ref, cb_ref, wq_ref, wk_ref,
                  wif_ref, wift_ref, bif_ref, bift_ref, tril_ref, triu_ref,
                  h_ref, c_out_ref, m_out_ref, xbuf_ref, c_ref, m_ref, *, L):
    j = pl.program_id(1)
    lo = CONV_PAD - (CONV_W - 1)

    @pl.when(j == 0)
    def _():
        xbuf_ref[lo:CONV_PAD, :] = conv0_ref[0]
        c_ref[...] = c0_ref[0]
        m_ref[...] = m0_ref[0]

    xbuf_ref[CONV_PAD:CONV_PAD + L, :] = cc_ref[0]
    bf_round = lambda a: a.astype(BF16).astype(F32)
    u = cb_ref[...]
    for w in range(CONV_W):
        u = u + bf_round(xbuf_ref[lo + w:lo + w + L, :]) * bf_round(cw_ref[w:w + 1, :])
    tail = xbuf_ref[L + lo:L + CONV_PAD, :]
    xbuf_ref[lo:CONV_PAD, :] = tail
    ub = (u * _sigmoid(u)).astype(BF16)
    qc = _dot(ub, wq_ref[...])
    kc = _dot(ub, wk_ref[...]) * (HD_C ** -0.5)
    vc = vc_ref[0]
    feats = jnp.concatenate([qc, kc, vc], axis=1).astype(BF16)
    gates = _dot(feats, wif_ref[...]) + bif_ref[...]
    gates_t = _dot_nt(wift_ref[...], feats) + bift_ref[...]
    lf = _log_sigmoid(gates)
    lf_t = _log_sigmoid(gates_t)
    fc = _dot(tril_ref[...], lf, HI)
    fc_t = _dot(lf_t, triu_ref[...], HI)
    causal = lax.broadcasted_iota(I32, (L, L), 0) >= lax.broadcasted_iota(I32, (L, L), 1)
    lane = lax.broadcasted_iota(I32, (L, LANE), 1)
    lane1 = lax.broadcasted_iota(I32, (1, LANE), 1)
    m_prev = m_ref[...]
    m_next = m_prev
    qcb, kcb, vcb = qc.astype(BF16), kc.astype(BF16), vc.astype(BF16)
    for h in range(H_C):
        sl = slice(LANE * h, LANE * (h + 1))
        fcol = fc[:, H_C + h:H_C + h + 1]
        icol = gates[:, h:h + 1]
        m0 = m_prev[:, h:h + 1]
        a = fcol + m0
        dm = jnp.where(causal, fcol - fc_t[H_C + h:H_C + h + 1, :] + gates_t[h:h + 1, :], -jnp.inf)
        m = jnp.maximum(a, jnp.max(dm, axis=1, keepdims=True))
        w_in = jnp.exp(dm - m)
        w_st = jnp.exp(a - m)
        vh = vcb[:, sl]
        kh = kcb[:, sl]
        qk = _dot_nt(qcb[:, sl], kh) * w_in
        c_old = c_ref[h]
        qc0 = _dot(qcb[:, sl], c_old.astype(BF16))
        num = _dot(qk.astype(BF16), vh) + w_st * qc0
        den = jnp.sum(qk, axis=1, keepdims=True) + w_st * qc0[:, HD_C:HD_C + 1]
        hh = num / jnp.maximum(jnp.abs(den), jnp.exp(-m))
        h_ref[0, :, sl] = jnp.where(lane < HD_C, hh, 0.0)
        m_last = m[L - 1:L, :]
        w_last = jnp.exp(fcol[L - 1:L, :] - fcol + icol - m_last)
        g_last = jnp.exp(a[L - 1:L, :] - m_last)
        w_ncol = jnp.where(lane == HD_C, w_last, 0.0).astype(BF16)
        c_ref[h] = g_last * c_old + _dot_tn((kc[:, sl] * w_last).astype(BF16), vh) + _dot_tn(kh, w_ncol)
        m_next = jnp.where(lane1 == h, m_last, m_next)
    m_ref[...] = m_next

    @pl.when(j == pl.num_programs(1) - 1)
    def _():
        c_out_ref[0] = c_ref[...]
        m_out_ref[0] = m_ref[...]


def _mlstm(cc, vc, conv0, c0, m0, mw, L):
    b, t, _ = cc.shape
    nc = t // L
    tril = jnp.asarray(np.tril(np.ones((L, L), np.float32)))
    tok = lambda w: pl.BlockSpec((1, L, w), lambda bi, j: (bi, j, 0))
    per_b = lambda shape: pl.BlockSpec((1,) + shape, lambda bi, j: (bi,) + (0,) * len(shape))
    ws = (mw["cw"], mw["cb"], mw["wq"], mw["wk"], mw["wif"], mw["wift"], mw["bif"], mw["bift"], tril, tril.T)
    return pl.pallas_call(
        functools.partial(_mlstm_kernel, L=L),
        grid=(b, nc),
        in_specs=[tok(W_C), tok(HP), per_b((CONV_W - 1, W_C)), per_b((H_C, LANE, LANE)), per_b((1, LANE))]
                 + [_full(w.shape) for w in ws],
        out_specs=[tok(HP), per_b((H_C, LANE, LANE)), per_b((1, LANE))],
        out_shape=[jax.ShapeDtypeStruct((b, t, HP), F32), jax.ShapeDtypeStruct((b, H_C, LANE, LANE), F32),
                   jax.ShapeDtypeStruct((b, 1, LANE), F32)],
        scratch_shapes=[pltpu.VMEM((L + CONV_PAD, W_C), F32), pltpu.VMEM((H_C, LANE, LANE), F32),
                        pltpu.VMEM((1, LANE), F32)],
        compiler_params=_params(("parallel", "arbitrary")),
        name="mlstm",
    )(cc, vc, conv0, c0, m0, *ws)


def _block_diag_heads(w):
    h, d, _ = w.shape
    out = jnp.zeros((h * d, h * LANE), w.dtype)
    for i in range(h):
        out = out.at[i * d:(i + 1) * d, i * LANE:i * LANE + d].set(w[i])
    return out


def _mlstm_weights(conv_w, conv_b, w_mq, w_mk, w_if, b_if):
    pad_rows = lambda w: _pad_heads(w.T, H_C, HD_C).T
    wif = jnp.concatenate([pad_rows(w_if[i * W_C:(i + 1) * W_C]) for i in range(3)], axis=0)
    return dict(cw=conv_w, cb=conv_b.reshape(1, -1),
                wq=_block_diag_heads(w_mq).astype(BF16), wk=_block_diag_heads(w_mk).astype(BF16),
                wif=_pad_last(wif, LANE).astype(BF16), wift=jnp.pad(wif.T, ((0, 8), (0, 0))).astype(BF16),
                bif=_pad_last(b_if.reshape(1, -1), LANE), bift=jnp.pad(b_if.reshape(-1, 1), ((0, 8), (0, 0))))


def _head_rms(x, g):
    outs = []
    for h in range(4):
        seg = x[:, LANE * h:LANE * (h + 1)]
        ms = jnp.sum(seg * seg, axis=1, keepdims=True) * (1.0 / HD_C)
        outs.append(seg * lax.rsqrt(ms + LN_EPS))
    return jnp.concatenate(outs, axis=1) * g


def _route(s, sel):
    col = lambda a, e: a[:, e:e + 1]
    gs = []
    for g in range(N_GROUPS):
        c = [col(sel, g * EXPERTS_PER_GROUP + i) for i in range(EXPERTS_PER_GROUP)]
        best = None
        for i in range(EXPERTS_PER_GROUP):
            for j in range(i + 1, EXPERTS_PER_GROUP):
                pair = c[i] + c[j]
                best = pair if best is None else jnp.maximum(best, pair)
        gs.append(best)
    lane = lax.broadcasted_iota(I32, s.shape, 1)
    picked, wsum = [], None
    for g in range(N_GROUPS):
        is_best = None
        for g2 in range(N_GROUPS):
            if g2 == g:
                continue
            c2 = (gs[g] > gs[g2]) if g2 < g else (gs[g] >= gs[g2])
            is_best = c2 if is_best is None else (is_best & c2)
        for i in range(EXPERTS_PER_GROUP):
            e = g * EXPERTS_PER_GROUP + i
            rank = jnp.zeros(gs[g].shape, I32)
            for i2 in range(EXPERTS_PER_GROUP):
                if i2 == i:
                    continue
                e2 = g * EXPERTS_PER_GROUP + i2
                ahead = (col(sel, e2) > col(sel, e)) | ((col(sel, e2) == col(sel, e)) & (i2 < i))
                rank = rank + jnp.where(ahead, 1, 0)
            take = is_best & (rank < 2)
            w = jnp.where(take, col(s, e), 0.0)
            picked.append(w)
            wsum = w if wsum is None else wsum + w
    gate = jnp.zeros(s.shape, F32)
    for e in range(N_EXPERTS):
        gate = jnp.where(lane == e, picked[e] / wsum, gate)
    return gate


def _ffn_kernel(x_ref, ya_ref, ob_ref, rb_ref, hc_ref, oc_ref, gb_ref, gc_ref, woa_ref, wob_ref, woc_ref,
                l1g_ref, l1b_ref, wr_ref, br_ref, wg_ref, wu_ref, wd_ref, l2g_ref, l2b_ref,
                o_ref, x1_ref, x1b_ref, gate_ref, acc_ref):
    e = pl.program_id(1)

    @pl.when(e == 0)
    def _():
        rb = rb_ref[...]
        yb = _head_rms(ob_ref[...], gb_ref[...]) * (rb * _sigmoid(rb))
        yc = _head_rms(_sigmoid(oc_ref[...]) * hc_ref[...], gc_ref[...])
        y = (_dot(ya_ref[...].astype(BF16), woa_ref[...]) + _dot(yb.astype(BF16), wob_ref[...])
             + _dot(yc.astype(BF16), woc_ref[...]))
        x1 = _layer_norm(ALPHA * x_ref[...] + y, l1g_ref[...], l1b_ref[...])
        x1_ref[...] = x1
        x1b_ref[...] = x1.astype(BF16)
        s = _sigmoid(_dot(x1.astype(BF16), wr_ref[...]))
        gate_ref[...] = _route(s, s + br_ref[...])
        acc_ref[...] = jnp.zeros(acc_ref.shape, F32)

    xb = x1b_ref[...]
    hg = _dot(xb, wg_ref[0])
    hu = _dot(xb, wu_ref[0])
    gates = gate_ref[...]
    lane = lax.broadcasted_iota(I32, gates.shape, 1)
    gcol = jnp.sum(jnp.where(lane == e, gates, 0.0), axis=1, keepdims=True)
    hid = (hg * _sigmoid(hg) * hu).astype(BF16)
    acc_ref[...] += gcol * _dot(hid, wd_ref[0])

    @pl.when(e == N_EXPERTS - 1)
    def _():
        o_ref[...] = _layer_norm(ALPHA * x1_ref[...] + acc_ref[...], l2g_ref[...], l2b_ref[...])


def _ffn(x, ya, ob, rb, hc, oc, fw, tm):
    n = x.shape[0]
    row = lambda w: pl.BlockSpec((tm, w), lambda i, e: (i, 0))
    full = lambda a: pl.BlockSpec(a.shape, lambda i, e: (0,) * a.ndim)
    exp_in = pl.BlockSpec((1, D_MODEL, D_EXPERT), lambda i, e: (e, 0, 0))
    exp_out = pl.BlockSpec((1, D_EXPERT, D_MODEL), lambda i, e: (e, 0, 0))
    small = (fw["gb"], fw["gc"], fw["woa"], fw["wob"], fw["woc"], fw["l1g"], fw["l1b"], fw["wr"], fw["br"])
    return pl.pallas_call(
        _ffn_kernel,
        grid=(n // tm, N_EXPERTS),
        in_specs=[row(D_MODEL), row(W_A), row(HP), row(HP), row(HP), row(HP)] + [full(a) for a in small]
                 + [exp_in, exp_in, exp_out, full(fw["l2g"]), full(fw["l2b"])],
        out_specs=row(D_MODEL),
        out_shape=jax.ShapeDtypeStruct((n, D_MODEL), F32),
        scratch_shapes=[pltpu.VMEM((tm, D_MODEL), F32), pltpu.VMEM((tm, D_MODEL), BF16),
                        pltpu.VMEM((tm, LANE), F32), pltpu.VMEM((tm, D_MODEL), F32)],
        compiler_params=_params(("parallel", "arbitrary")),
        name="ffn",
    )(x, ya, ob, rb, hc, oc, *small, fw["wg"], fw["wu"], fw["wd"], fw["l2g"], fw["l2b"])


def _pad_head_rows(w, heads, d):
    return _pad_heads(w.T, heads, d).T


def _ffn_weights(g_gla, g_mlstm, w_out, ln1_g, ln1_b, w_router, b_router, wg, wu, wd, ln2_g, ln2_b):
    r = lambda a: a.reshape(1, -1)
    return dict(gb=_pad_heads(r(g_gla), H_B, DV_B), gc=_pad_heads(r(g_mlstm), H_C, HD_C),
                woa=w_out[:W_A].astype(BF16),
                wob=_pad_head_rows(w_out[W_A:W_A + W_B], H_B, DV_B).astype(BF16),
                woc=_pad_head_rows(w_out[W_A + W_B:], H_C, HD_C).astype(BF16),
                l1g=r(ln1_g), l1b=r(ln1_b), wr=_pad_last(w_router, LANE).astype(BF16), br=_pad_last(r(b_router), LANE),
                wg=wg.astype(BF16), wu=wu.astype(BF16), wd=wd.astype(BF16), l2g=r(ln2_g), l2b=r(ln2_b))


def _pick_tile(n, pref):
    t = pref
    while n % t:
        t //= 2
    return t


def _unpad_heads(a, heads, d):
    return a.reshape(*a.shape[:-1], heads, LANE)[..., :d]


def _layer(x, bsz, t, pw, mw, fw, past):
    n = bsz * t
    p = _proj(x, pw, _pick_tile(n, 256))
    sh = lambda a: a.reshape(bsz, t, a.shape[-1])
    ka, va, ki = sh(p["ka"]), sh(p["va"]), sh(p["ki"])
    if past is None:
        tq = _pick_tile(t, 128)
        ya = _dsa(sh(p["qab"]), sh(p["qib"]), sh(p["wi"]), sh(p["kab"]), sh(p["vab"]), sh(p["kib"]),
                  q_offset=0, kv_len=t, tq=tq, tk=_pick_tile(t, 512))
        s0t = jnp.zeros((bsz, H_B, LANE, LANE), F32)
        conv0 = jnp.zeros((bsz, CONV_W - 1, W_C), F32)
        c0 = jnp.zeros((bsz, H_C, LANE, LANE), F32)
        m0 = jnp.zeros((bsz, 1, LANE), F32)
        L = _pick_tile(t, CHUNK)
    else:
        ck, cv, cki, s_gla, c_m, n_m, m_m, conv0 = past
        plen = ck.shape[1]
        cat = lambda c, new: jnp.concatenate([c.reshape(bsz, plen, -1).astype(BF16), new], axis=1)
        kv_len = plen + t
        tk = _key_tile(kv_len)
        ya = _dsa(sh(p["qab"]), sh(p["qib"]), sh(p["wi"]), cat(ck, sh(p["kab"])), cat(cv, sh(p["vab"])),
                  cat(cki, sh(p["kib"])), q_offset=plen, kv_len=kv_len, tq=t, tk=tk)
        s0t = jnp.pad(s_gla.transpose(0, 1, 3, 2), ((0, 0), (0, 0), (0, LANE - DV_B), (0, LANE - DK_B)))
        c0 = jnp.pad(c_m, ((0, 0), (0, 0), (0, LANE - HD_C), (0, LANE - HD_C)))
        c0 = c0.at[:, :, :HD_C, HD_C].set(n_m)
        m0 = _pad_last(m_m, LANE).reshape(bsz, 1, LANE)
        L = t
    ob, s_new_t = _gla(sh(p["qb"]), sh(p["kb"]), sh(p["vb"]), sh(p["lg"]), s0t, L)
    hc, c_new, m_new = _mlstm(sh(p["cc"]), sh(p["vc"]), conv0, c0, m0, mw, L)
    x_new = _ffn(x, ya.reshape(n, W_A), ob.reshape(n, HP), p["rb"], hc.reshape(n, HP), p["oc"], fw,
                 _pick_tile(n, 512))
    xc = jnp.concatenate([conv0, sh(p["cc"])], axis=1)
    state = (ka.reshape(bsz, t, H_A, HD_A), va.reshape(bsz, t, H_A, HD_A), ki,
             s_new_t[:, :, :DV_B, :DK_B].transpose(0, 1, 3, 2),
             c_new[:, :, :HD_C, :HD_C], c_new[:, :, :HD_C, HD_C], m_new[:, 0, :H_C],
             xc[:, -(CONV_W - 1):])
    return x_new, state


def _key_tile(kv_len):
    best = None
    for tk in range(16, 513, 16):
        if kv_len % tk == 0:
            best = tk
    assert best is not None, "key length must have a divisor that is a multiple of 16"
    return best


def kernel(x_prompt, x_sample, cache_k, cache_v, cache_kidx, state_gla, state_mlstm_C, state_mlstm_n, state_mlstm_m, state_conv, ln0_g, ln0_b, w_in, w_gla_a2, b_gla_a, g_gla, conv_w, conv_b, w_mq, w_mk, w_if, b_if, g_mlstm, w_out, ln1_g, ln1_b, w_router, b_router, w_exp_gate, w_exp_up, w_exp_down, ln2_g, ln2_b):
    bp, tp, _ = x_prompt.shape
    bs, ts, _ = x_sample.shape
    depth = w_in.shape[0]
    xp = _ln0(x_prompt.reshape(bp * tp, D_MODEL), ln0_g, ln0_b, _pick_tile(bp * tp, 512))
    xs = _ln0(x_sample.reshape(bs * ts, D_MODEL), ln0_g, ln0_b, _pick_tile(bs * ts, 512))
    sp, ss = [], []
    for l in range(depth):
        pw = _proj_weights(w_in[l], w_gla_a2[l], b_gla_a[l])
        mw = _mlstm_weights(conv_w[l], conv_b[l], w_mq[l], w_mk[l], w_if[l], b_if[l])
        fw = _ffn_weights(g_gla[l], g_mlstm[l], w_out[l], ln1_g[l], ln1_b[l], w_router, b_router,
                          w_exp_gate[l], w_exp_up[l], w_exp_down[l], ln2_g[l], ln2_b[l])
        xp, st_p = _layer(xp, bp, tp, pw, mw, fw, None)
        past = (cache_k[l], cache_v[l], cache_kidx[l], state_gla[l], state_mlstm_C[l], state_mlstm_n[l],
                state_mlstm_m[l], state_conv[l])
        xs, st_s = _layer(xs, bs, ts, pw, mw, fw, past)
        sp.append(st_p)
        ss.append(st_s)
    stk = lambda sts, i: jnp.stack([s[i] for s in sts])
    return (xp.reshape(bp, tp, D_MODEL), xs.reshape(bs, ts, D_MODEL),
            *[stk(sp, i) for i in range(8)], *[stk(ss, i) for i in range(8)])
```
